```python
import jax, jax.numpy as jnp
from jax import lax
import numpy as np

D_MODEL = 1024
BATCH = 8
SEQ = 2048
DEPTH = 2
DEC_BATCH = 16
DEC_SEQ = 16
PAST_LEN = 1024

CHUNK = 64
MIX_W = D_MODEL // 2
N_BRANCH = 4
CONV_W = 4
EPS = 1e-6
S5_GROUP = 16
S5_GROUPS = MIX_W // S5_GROUP
S5_STATE = 64
GDN_HEADS = 4
GDN_DK = MIX_W // GDN_HEADS
GDN_DV = MIX_W // GDN_HEADS
LRU_BLOCKS = 8
LRU_BW = MIX_W // LRU_BLOCKS
LRU_C = 8.0
SSD_HEADS = 8
SSD_HEADDIM = MIX_W // SSD_HEADS
SSD_GROUPS = 2
SSD_STATE = 128
SSD_CONV_DIM = MIX_W + 2 * SSD_GROUPS * SSD_STATE
N_EXPERTS = 16
N_EXPERT_GROUPS = 4
EXPERTS_PER_GROUP = N_EXPERTS // N_EXPERT_GROUPS
TOP_K = 2
D_EXPERT = D_MODEL // 2
MOE_BLOCK = 128
IN_SPLITS = (MIX_W, 3 * MIX_W, MIX_W, GDN_HEADS, GDN_HEADS, MIX_W, MIX_W, MIX_W, SSD_CONV_DIM, SSD_HEADS, N_BRANCH * D_MODEL)
IN_COLS = sum(IN_SPLITS)

kernel_name = 'hybrid_streaming_encoder_step'


def _rmsnorm(x, w):
    xf = x.astype(jnp.float32)
    y = xf * lax.rsqrt(jnp.mean(xf * xf, axis=-1, keepdims=True) + EPS)
    return y.astype(x.dtype) * w


def _split(x, sizes):
    return jnp.split(x, np.cumsum(sizes)[:-1].tolist(), axis=-1)


def _causal_conv(x, buf, w, b):
    L = x.shape[1]
    xp = jnp.concatenate([buf.astype(x.dtype), x], axis=1)
    out = xp[:, 0:L] * w[0]
    for k in range(1, CONV_W):
        out = out + xp[:, k:k + L] * w[k]
    if b is not None:
        out = out + b
    return out, xp[:, L:]


def _chunk(a, axis, c):
    s = a.shape
    a = a.reshape(s[:axis] + (s[axis] // c, c) + s[axis + 1:])
    return jnp.moveaxis(a, axis, 0)


def _unchunk(a, axis):
    a = jnp.moveaxis(a, 0, axis)
    s = a.shape
    return a.reshape(s[:axis] + (s[axis] * s[axis + 1],) + s[axis + 2:])


def _affine_combine(left, right):
    a1, b1 = left
    a2, b2 = right
    return (a2 * a1, a2 * b1 + b2)


def _complex_affine_combine(left, right):
    a1r, a1i, b1r, b1i = left
    a2r, a2i, b2r, b2i = right
    return (a2r * a1r - a2i * a1i, a2r * a1i + a2i * a1r,
            a2r * b1r - a2i * b1i + b2r, a2r * b1i + a2i * b1r + b2i)


def _s5_mixer(u, st_re, st_im, P):
    f32 = jnp.float32
    bsz, L, _ = u.shape
    uf = u.astype(f32)
    ug = uf.reshape(bsz, L, S5_GROUPS, S5_GROUP)
    lam_re = P['s5_a_re'].astype(f32)
    lam_im = P['s5_a_im'].astype(f32)
    dt = jnp.exp(P['s5_log_dt'].astype(f32))[:, None]
    mag = jnp.exp(lam_re * dt)
    lb_re = mag * jnp.cos(lam_im * dt)
    lb_im = mag * jnp.sin(lam_im * dt)
    den = lam_re * lam_re + lam_im * lam_im
    f_re = ((lb_re - 1.0) * lam_re + lb_im * lam_im) / den
    f_im = (lb_im * lam_re - (lb_re - 1.0) * lam_im) / den
    b_re = P['s5_b_re'].astype(f32)
    b_im = P['s5_b_im'].astype(f32)
    bb_re = f_re[..., None] * b_re - f_im[..., None] * b_im
    bb_im = f_re[..., None] * b_im + f_im[..., None] * b_re
    bu_re = jnp.einsum('gpc,blgc->blgp', bb_re, ug)
    bu_im = jnp.einsum('gpc,blgc->blgp', bb_im, ug)
    x0_re = st_re.astype(f32)
    x0_im = st_im.astype(f32)
    bu_re = bu_re.at[:, 0].add(lb_re * x0_re - lb_im * x0_im)
    bu_im = bu_im.at[:, 0].add(lb_re * x0_im + lb_im * x0_re)
    a_re = jnp.broadcast_to(lb_re, bu_re.shape)
    a_im = jnp.broadcast_to(lb_im, bu_im.shape)
    _, _, xs_re, xs_im = lax.associative_scan(_complex_affine_combine, (a_re, a_im, bu_re, bu_im), axis=1)
    y = (jnp.einsum('gcp,blgp->blgc', P['s5_c_re'].astype(f32), xs_re)
         - jnp.einsum('gcp,blgp->blgc', P['s5_c_im'].astype(f32), xs_im))
    y = y.reshape(bsz, L, MIX_W) + P['s5_d'].astype(f32) * uf
    y = jax.nn.gelu(y)
    y = y * jax.nn.sigmoid(y @ P['s5_w_glu'].astype(f32))
    return y.astype(u.dtype), xs_re[:, -1].astype(st_re.dtype), xs_im[:, -1].astype(st_im.dtype)


def _gdn_chunked(q, k, v, g, beta, s0):
    L = q.shape[2]
    c = min(CHUNK, L)
    incl = jnp.tril(jnp.ones((c, c), dtype=bool))
    strict = jnp.tril(jnp.ones((c, c), dtype=bool), k=-1)
    eye = jnp.eye(c, dtype=jnp.float32)

    def step(S, inp):
        qc, kc, vc, gc, bc = inp
        gcum = jnp.cumsum(gc, axis=-1)
        decay = jnp.exp(jnp.where(incl, gcum[..., :, None] - gcum[..., None, :], -jnp.inf))
        kk = jnp.einsum('bhtk,bhsk->bhts', kc, kc)
        lhs = eye + jnp.where(strict, bc[..., :, None] * kk * decay, 0.0)
        rhs = jnp.concatenate([vc * bc[..., None], kc * (bc * jnp.exp(gcum))[..., None]], axis=-1)
        sol = lax.linalg.triangular_solve(lhs, rhs, left_side=True, lower=True, unit_diagonal=True)
        u = sol[..., :GDN_DV] - jnp.einsum('bhtk,bhkv->bhtv', sol[..., GDN_DV:], S)
        attn = jnp.einsum('bhtk,bhsk->bhts', qc, kc) * decay
        o = (jnp.einsum('bhtk,bhkv->bhtv', qc * jnp.exp(gcum)[..., None], S)
             + jnp.einsum('bhts,bhsv->bhtv', attn, u))
        k_end = kc * jnp.exp(gcum[..., -1:] - gcum)[..., None]
        S = S * jnp.exp(gcum[..., -1])[..., None, None] + jnp.einsum('bhsk,bhsv->bhkv', k_end, u)
        return S, o

    xs = tuple(_chunk(a, 2, c) for a in (q, k, v, g, beta))
    s_fin, o = lax.scan(step, s0, xs)
    return _unchunk(o, 2), s_fin


def _gdn_mixer(qkv, z, a_raw, b_raw, s0, buf, P):
    f32 = jnp.float32
    bsz, L, _ = qkv.shape
    qkv_c, new_buf = _causal_conv(qkv, buf, P['gdn_conv_w'], None)
    qkv_c = jax.nn.silu(qkv_c.astype(f32))
    q, k, v = jnp.split(qkv_c, 3, axis=-1)

    def heads(t):
        return t.reshape(bsz, L, GDN_HEADS, -1).transpose(0, 2, 1, 3)

    q, k, v = heads(q), heads(k), heads(v)
    q = q * lax.rsqrt(jnp.sum(q * q, axis=-1, keepdims=True) + EPS) * (GDN_DK ** -0.5)
    k = k * lax.rsqrt(jnp.sum(k * k, axis=-1, keepdims=True) + EPS)
    beta = jax.nn.sigmoid(b_raw.astype(f32)).transpose(0, 2, 1)
    g = (-jnp.exp(P['gdn_a_log'].astype(f32))
         * jax.nn.softplus(a_raw.astype(f32) + P['gdn_dt_bias'].astype(f32))).transpose(0, 2, 1)
    o, s_fin = _gdn_chunked(q, k, v, g, beta, s0.astype(f32))
    o = o.transpose(0, 2, 1, 3)
    o = o * lax.rsqrt(jnp.mean(o * o, axis=-1, keepdims=True) + EPS) * P['gdn_norm_w'].astype(f32)
    o = o * jax.nn.silu(z.astype(f32)).reshape(bsz, L, GDN_HEADS, GDN_DV)
    return o.reshape(bsz, L, MIX_W).astype(qkv.dtype), s_fin.astype(s0.dtype), new_buf.astype(buf.dtype)


def _lru_mixer(xc, gate, h0, buf, P):
    f32 = jnp.float32
    bsz, L, _ = xc.shape
    xt, new_buf = _causal_conv(xc, buf, P['lru_conv_w'], P['lru_conv_b'])
    xf = xt.astype(f32)
    xb = xf.reshape(bsz, L, LRU_BLOCKS, LRU_BW)
    r = jax.nn.sigmoid(jnp.einsum('blnc,ncd->blnd', xb, P['lru_w_a'].astype(f32)).reshape(bsz, L, MIX_W)
                       + P['lru_b_a'].astype(f32))
    i = jax.nn.sigmoid(jnp.einsum('blnc,ncd->blnd', xb, P['lru_w_x'].astype(f32)).reshape(bsz, L, MIX_W)
                       + P['lru_b_x'].astype(f32))
    log_a = -LRU_C * r * jax.nn.softplus(-P['lru_lambda'].astype(f32))
    a = jnp.exp(log_a)
    bterm = jnp.sqrt(-jnp.expm1(2.0 * log_a)) * (i * xf)
    bterm = bterm.at[:, 0].add(a[:, 0] * h0.astype(f32))
    _, h = lax.associative_scan(_affine_combine, (a, bterm), axis=1)
    y = h * jax.nn.gelu(gate.astype(f32))
    return y.astype(xc.dtype), h[:, -1].astype(h0.dtype), new_buf.astype(buf.dtype)


def _ssd_chunked(x, dt, la, bm, cm, h0):
    L = x.shape[1]
    c = min(CHUNK, L)
    incl = jnp.tril(jnp.ones((c, c), dtype=bool))[None, :, :, None]

    def step(h, inp):
        xc, dtc, lac, bc, cc = inp
        gcum = jnp.cumsum(lac, axis=1)
        decay = jnp.exp(jnp.where(incl, gcum[:, :, None, :] - gcum[:, None, :, :], -jnp.inf))
        scores = jnp.einsum('bthn,bshn->btsh', cc, bc) * decay
        y = jnp.einsum('btsh,bshp->bthp', scores, xc * dtc[..., None])
        y = y + jnp.einsum('bthn,bhpn->bthp', cc, h) * jnp.exp(gcum)[..., None]
        w_end = jnp.exp(gcum[:, -1:] - gcum) * dtc
        h = h * jnp.exp(gcum[:, -1])[:, :, None, None] + jnp.einsum('bsh,bshp,bshn->bhpn', w_end, xc, bc)
        return h, y

    xs = tuple(_chunk(a, 1, c) for a in (x, dt, la, bm, cm))
    h_fin, y = lax.scan(step, h0, xs)
    return _unchunk(y, 1), h_fin


def _ssd_mixer(z, xbc, dt_raw, h0, buf, P):
    f32 = jnp.float32
    bsz, L, _ = xbc.shape
    xbc_c, new_buf = _causal_conv(xbc, buf, P['ssd_conv_w'], P['ssd_conv_b'])
    xbc_c = jax.nn.silu(xbc_c.astype(f32))
    xs, bm, cm = _split(xbc_c, (MIX_W, SSD_GROUPS * SSD_STATE, SSD_GROUPS * SSD_STATE))
    xs = xs.reshape(bsz, L, SSD_HEADS, SSD_HEADDIM)
    rep = SSD_HEADS // SSD_GROUPS
    bm = jnp.repeat(bm.reshape(bsz, L, SSD_GROUPS, SSD_STATE), rep, axis=2)
    cm = jnp.repeat(cm.reshape(bsz, L, SSD_GROUPS, SSD_STATE), rep, axis=2)
    dt = jax.nn.softplus(dt_raw.astype(f32) + P['ssd_dt_bias'].astype(f32))
    la = -jnp.exp(P['ssd_a_log'].astype(f32)) * dt
    y, h_fin = _ssd_chunked(xs, dt, la, bm, cm, h0.astype(f32))
    y = y + P['ssd_d'].astype(f32)[:, None] * xs
    y = y.reshape(bsz, L, MIX_W) * jax.nn.silu(z.astype(f32))
    y = y * lax.rsqrt(jnp.mean(y * y, axis=-1, keepdims=True) + EPS) * P['ssd_norm_w'].astype(f32)
    return y.astype(xbc.dtype), h_fin.astype(h0.dtype), new_buf.astype(buf.dtype)


def _token_mixers(h, st, P):
    bsz, L, _ = h.shape
    s5_re0, s5_im0, gdn_s0, gdn_buf0, lru_h0, lru_buf0, ssd_h0, ssd_buf0 = st
    proj = h @ P['w_in']
    u_a, qkv_b, z_b, a_b, b_b, x_c, gate_c, z_d, xbc_d, dt_d, gate_raw = _split(proj, IN_SPLITS)
    y_a, s5_re, s5_im = _s5_mixer(u_a, s5_re0, s5_im0, P)
    y_b, gdn_s, gdn_buf = _gdn_mixer(qkv_b, z_b, a_b, b_b, gdn_s0, gdn_buf0, P)
    y_c, lru_h, lru_buf = _lru_mixer(x_c, gate_c, lru_h0, lru_buf0, P)
    y_d, ssd_h, ssd_buf = _ssd_mixer(z_d, xbc_d, dt_d, ssd_h0, ssd_buf0, P)
    ys = jnp.stack([y_a, y_b, y_c, y_d], axis=2)
    branches = jnp.einsum('blkm,kmd->blkd', ys, P['w_branch'])
    gates = jax.nn.sigmoid(gate_raw).reshape(bsz, L, N_BRANCH, D_MODEL)
    out = jnp.sum(gates * branches, axis=2) @ P['w_out']
    return out, (s5_re, s5_im, gdn_s, gdn_buf, lru_h, lru_buf, ssd_h, ssd_buf)


def _grouped_moe(h, router_w, router_bias, w_gate, w_up, w_down):
    f32 = jnp.float32
    T, D = h.shape
    scores = jax.nn.sigmoid(h.astype(f32) @ router_w.astype(f32))
    sel = (scores + router_bias.astype(f32)).reshape(T, N_EXPERT_GROUPS, EXPERTS_PER_GROUP)
    grp_score = jnp.sum(lax.top_k(sel, TOP_K)[0], axis=-1)
    g_idx = jnp.argmax(grp_score, axis=-1)
    in_grp = jnp.take_along_axis(sel, g_idx[:, None, None], axis=1)[:, 0]
    expert = g_idx[:, None] * EXPERTS_PER_GROUP + lax.top_k(in_grp, TOP_K)[1]
    wts = jnp.take_along_axis(scores, expert, axis=-1)
    wts = wts / jnp.sum(wts, axis=-1, keepdims=True)
    n_assign = T * TOP_K
    flat_e = expert.reshape(-1)
    order = jnp.argsort(flat_e)
    e_s = flat_e[order]
    t_s = (jnp.arange(n_assign) // TOP_K)[order]
    w_s = wts.reshape(-1)[order]
    counts = jnp.bincount(flat_e, length=N_EXPERTS)
    start = jnp.cumsum(counts) - counts
    padded = (counts + MOE_BLOCK - 1) // MOE_BLOCK * MOE_BLOCK
    pend = jnp.cumsum(padded)
    pos = (pend - padded)[e_s] + jnp.arange(n_assign) - start[e_s]
    n_blocks = -(-(n_assign + N_EXPERTS * (MOE_BLOCK - 1)) // MOE_BLOCK)
    block_expert = jnp.minimum(jnp.searchsorted(pend, jnp.arange(n_blocks) * MOE_BLOCK, side='right'), N_EXPERTS - 1)
    buf = jnp.zeros((n_blocks * MOE_BLOCK, D), h.dtype).at[pos].set(h[t_s])

    def expert_block(args):
        xb, e = args
        return (jax.nn.silu(xb @ w_gate[e]) * (xb @ w_up[e])) @ w_down[e]

    yb = lax.map(expert_block, (buf.reshape(n_blocks, MOE_BLOCK, D), block_expert)).reshape(n_blocks * MOE_BLOCK, D)
    return jnp.zeros_like(h).at[t_s].add(yb[pos] * w_s[:, None].astype(h.dtype))


def _layer(x, c, st, P, router_w, router_bias):
    bsz, L, _ = x.shape
    mod = jax.nn.silu(c) @ P['w_ada'] + P['b_ada']
    sh_m, sc_m, g_m, sh_f, sc_f, g_f = jnp.split(mod[:, None, :], 6, axis=-1)
    h = _rmsnorm(x, P['norm_mix_w']) * (1.0 + sc_m) + sh_m
    mix, new_st = _token_mixers(h, st, P)
    x = x + g_m * mix
    h = _rmsnorm(x, P['norm_moe_w']) * (1.0 + sc_f) + sh_f
    f = _grouped_moe(h.reshape(bsz * L, D_MODEL), router_w, router_bias, P['moe_w_gate'], P['moe_w_up'], P['moe_w_down'])
    x = x + g_f * f.reshape(bsz, L, D_MODEL)
    return x, new_st


def _zero_states(bsz, dtype):
    return (jnp.zeros((bsz, S5_GROUPS, S5_STATE), dtype),
            jnp.zeros((bsz, S5_GROUPS, S5_STATE), dtype),
            jnp.zeros((bsz, GDN_HEADS, GDN_DK, GDN_DV), dtype),
            jnp.zeros((bsz, CONV_W - 1, 3 * MIX_W), dtype),
            jnp.zeros((bsz, MIX_W), dtype),
            jnp.zeros((bsz, CONV_W - 1, MIX_W), dtype),
            jnp.zeros((bsz, SSD_HEADS, SSD_HEADDIM, SSD_STATE), dtype),
            jnp.zeros((bsz, CONV_W - 1, SSD_CONV_DIM), dtype))


def _trunk(x, c, layer_states, layer_params, router_w, router_bias, norm_final_w):
    new_states = []
    for l in range(DEPTH):
        x, st = _layer(x, c, layer_states[l], layer_params[l], router_w, router_bias)
        new_states.append(st)
    stacked = tuple(jnp.stack([new_states[l][i] for l in range(DEPTH)]) for i in range(8))
    return _rmsnorm(x, norm_final_w), stacked


def setup_inputs(seed: int = 0) -> dict:
    key = jax.random.key(seed)
    ks = iter(jax.random.split(key, 64))
    f32 = jnp.float32

    def nrm(shape, scale):
        return jax.random.normal(next(ks), shape, f32) * scale

    def unif(shape, lo, hi):
        return jax.random.uniform(next(ks), shape, f32, lo, hi)

    def gain(shape):
        return 1.0 + nrm(shape, 0.01)

    def dt_bias(shape):
        dt = jnp.exp(unif(shape, np.log(1e-3), np.log(1e-1)))
        return dt + jnp.log(-jnp.expm1(-dt))

    lru_u = unif((DEPTH, MIX_W), 0.9, 0.999) ** (1.0 / LRU_C)
    return {
        'x_prompt': nrm((BATCH, SEQ, D_MODEL), 1.0),
        'x_sample': nrm((DEC_BATCH, DEC_SEQ, D_MODEL), 1.0),
        'c_prompt': nrm((BATCH, D_MODEL), 1.0),
        'c_sample': nrm((DEC_BATCH, D_MODEL), 1.0),
        'state_s5_re': nrm((DEPTH, DEC_BATCH, S5_GROUPS, S5_STATE), 0.1),
        'state_s5_im': nrm((DEPTH, DEC_BATCH, S5_GROUPS, S5_STATE), 0.1),
        'state_gdn': nrm((DEPTH, DEC_BATCH, GDN_HEADS, GDN_DK, GDN_DV), 0.1),
        'cache_gdn_conv': nrm((DEPTH, DEC_BATCH, CONV_W - 1, 3 * MIX_W), 1.0),
        'state_lru': nrm((DEPTH, DEC_BATCH, MIX_W), 0.5),
        'cache_lru_conv': nrm((DEPTH, DEC_BATCH, CONV_W - 1, MIX_W), 1.0),
        'state_ssd': nrm((DEPTH, DEC_BATCH, SSD_HEADS, SSD_HEADDIM, SSD_STATE), 0.1),
        'cache_ssd_conv': nrm((DEPTH, DEC_BATCH, CONV_W - 1, SSD_CONV_DIM), 1.0),
        'norm_mix_w': gain((DEPTH, D_MODEL)),
        'norm_moe_w': gain((DEPTH, D_MODEL)),
        'norm_final_w': gain((D_MODEL,)),
        'w_ada': nrm((DEPTH, D_MODEL, 6 * D_MODEL), 0.5 * D_MODEL ** -0.5),
        'b_ada': nrm((DEPTH, 6 * D_MODEL), 0.01),
        'w_in': nrm((DEPTH, D_MODEL, IN_COLS), D_MODEL ** -0.5),
        's5_a_re': -0.5 + nrm((DEPTH, S5_GROUPS, S5_STATE), 0.01),
        's5_a_im': jnp.pi * jnp.arange(S5_STATE, dtype=f32) + nrm((DEPTH, S5_GROUPS, S5_STATE), 0.01),
        's5_b_re': nrm((DEPTH, S5_GROUPS, S5_STATE, S5_GROUP), (2 * S5_GROUP) ** -0.5),
        's5_b_im': nrm((DEPTH, S5_GROUPS, S5_STATE, S5_GROUP), (2 * S5_GROUP) ** -0.5),
        's5_c_re': nrm((DEPTH, S5_GROUPS, S5_GROUP, S5_STATE), (2 * S5_STATE) ** -0.5),
        's5_c_im': nrm((DEPTH, S5_GROUPS, S5_GROUP, S5_STATE), (2 * S5_STATE) ** -0.5),
        's5_log_dt': unif((DEPTH, S5_GROUPS), np.log(1e-3), np.log(1e-1)),
        's5_d': nrm((DEPTH, MIX_W), 1.0),
        's5_w_glu': nrm((DEPTH, MIX_W, MIX_W), MIX_W ** -0.5),
        'gdn_conv_w': nrm((DEPTH, CONV_W, 3 * MIX_W), CONV_W ** -0.5),
        'gdn_a_log': jnp.log(unif((DEPTH, GDN_HEADS), 1.0, 16.0)),
        'gdn_dt_bias': dt_bias((DEPTH, GDN_HEADS)),
        'gdn_norm_w': gain((DEPTH, GDN_DV)),
        'lru_conv_w': nrm((DEPTH, CONV_W, MIX_W), CONV_W ** -0.5),
        'lru_conv_b': nrm((DEPTH, MIX_W), 0.01),
        'lru_w_a': nrm((DEPTH, LRU_BLOCKS, LRU_BW, LRU_BW), LRU_BW ** -0.5),
        'lru_b_a': nrm((DEPTH, MIX_W), 0.01),
        'lru_w_x': nrm((DEPTH, LRU_BLOCKS, LRU_BW, LRU_BW), LRU_BW ** -0.5),
        'lru_b_x': nrm((DEPTH, MIX_W), 0.01),
        'lru_lambda': jnp.log(lru_u) - jnp.log1p(-lru_u),
        'ssd_conv_w': nrm((DEPTH, CONV_W, SSD_CONV_DIM), CONV_W ** -0.5),
        'ssd_conv_b': nrm((DEPTH, SSD_CONV_DIM), 0.01),
        'ssd_a_log': jnp.log(unif((DEPTH, SSD_HEADS), 1.0, 16.0)),
        'ssd_dt_bias': dt_bias((DEPTH, SSD_HEADS)),
        'ssd_d': gain((DEPTH, SSD_HEADS)),
        'ssd_norm_w': gain((DEPTH, MIX_W)),
        'w_branch': nrm((DEPTH, N_BRANCH, MIX_W, D_MODEL), MIX_W ** -0.5),
        'w_out': nrm((DEPTH, D_MODEL, D_MODEL), D_MODEL ** -0.5),
        'router_w': nrm((D_MODEL, N_EXPERTS), D_MODEL ** -0.5),
        'router_bias': nrm((N_EXPERTS,), 0.01),
        'moe_w_gate': nrm((DEPTH, N_EXPERTS, D_MODEL, D_EXPERT), D_MODEL ** -0.5),
        'moe_w_up': nrm((DEPTH, N_EXPERTS, D_MODEL, D_EXPERT), D_MODEL ** -0.5),
        'moe_w_down': nrm((DEPTH, N_EXPERTS, D_EXPERT, D_MODEL), D_EXPERT ** -0.5),
    }


def reference(x_prompt, x_sample, c_prompt, c_sample,
              state_s5_re, state_s5_im, state_gdn, cache_gdn_conv, state_lru, cache_lru_conv, state_ssd, cache_ssd_conv,
              norm_mix_w, norm_moe_w, norm_final_w, w_ada, b_ada, w_in,
              s5_a_re, s5_a_im, s5_b_re, s5_b_im, s5_c_re, s5_c_im, s5_log_dt, s5_d, s5_w_glu,
              gdn_conv_w, gdn_a_log, gdn_dt_bias, gdn_norm_w,
              lru_conv_w, lru_conv_b, lru_w_a, lru_b_a, lru_w_x, lru_b_x, lru_lambda,
              ssd_conv_w, ssd_conv_b, ssd_a_log, ssd_dt_bias, ssd_d, ssd_norm_w,
              w_branch, w_out,
              router_w, router_bias, moe_w_gate, moe_w_up, moe_w_down):
    layer_params = [
        dict(norm_mix_w=norm_mix_w[l], norm_moe_w=norm_moe_w[l], w_ada=w_ada[l], b_ada=b_ada[l], w_in=w_in[l],
             s5_a_re=s5_a_re[l], s5_a_im=s5_a_im[l], s5_b_re=s5_b_re[l], s5_b_im=s5_b_im[l],
             s5_c_re=s5_c_re[l], s5_c_im=s5_c_im[l], s5_log_dt=s5_log_dt[l], s5_d=s5_d[l], s5_w_glu=s5_w_glu[l],
             gdn_conv_w=gdn_conv_w[l], gdn_a_log=gdn_a_log[l], gdn_dt_bias=gdn_dt_bias[l], gdn_norm_w=gdn_norm_w[l],
             lru_conv_w=lru_conv_w[l], lru_conv_b=lru_conv_b[l], lru_w_a=lru_w_a[l], lru_b_a=lru_b_a[l],
             lru_w_x=lru_w_x[l], lru_b_x=lru_b_x[l], lru_lambda=lru_lambda[l],
             ssd_conv_w=ssd_conv_w[l], ssd_conv_b=ssd_conv_b[l], ssd_a_log=ssd_a_log[l],
             ssd_dt_bias=ssd_dt_bias[l], ssd_d=ssd_d[l], ssd_norm_w=ssd_norm_w[l],
             w_branch=w_branch[l], w_out=w_out[l],
             moe_w_gate=moe_w_gate[l], moe_w_up=moe_w_up[l], moe_w_down=moe_w_down[l])
        for l in range(DEPTH)]
    prompt_states = [_zero_states(x_prompt.shape[0], x_prompt.dtype) for _ in range(DEPTH)]
    cache_inputs = (state_s5_re, state_s5_im, state_gdn, cache_gdn_conv, state_lru, cache_lru_conv, state_ssd, cache_ssd_conv)
    sample_states = [tuple(s[l] for s in cache_inputs) for l in range(DEPTH)]
    y_prompt, p_states = _trunk(x_prompt, c_prompt, prompt_states, layer_params, router_w, router_bias, norm_final_w)
    y_sample, s_states = _trunk(x_sample, c_sample, sample_states, layer_params, router_w, router_bias, norm_final_w)
    p_s5_re, p_s5_im, p_gdn, p_gdn_conv, p_lru, p_lru_conv, p_ssd, p_ssd_conv = p_states
    s_s5_re, s_s5_im, s_gdn, s_gdn_conv, s_lru, s_lru_conv, s_ssd, s_ssd_conv = s_states
    return (y_prompt, y_sample,
            p_s5_re, p_s5_im, p_gdn, p_gdn_conv, p_lru, p_lru_conv, p_ssd, p_ssd_conv,
            s_s5_re, s_s5_im, s_gdn, s_gdn_conv, s_lru, s_lru_conv, s_ssd, s_ssd_conv)
```

```python
import functools
import math

import jax
import jax.numpy as jnp
from jax import lax
from jax.experimental import pallas as pl
from jax.experimental.pallas import tpu as pltpu

F32 = jnp.float32
BF16 = jnp.bfloat16

D_MODEL = 1024
DEPTH = 2
MIX_W = D_MODEL // 2
N_BRANCH = 4
CONV_W = 4
EPS = 1e-6
S5_GROUP = 16
S5_GROUPS = MIX_W // S5_GROUP
S5_STATE = 64
S5_FLAT = S5_GROUPS * S5_STATE
GDN_HEADS = 4
GDN_DK = MIX_W // GDN_HEADS
LRU_BLOCKS = 8
LRU_BW = MIX_W // LRU_BLOCKS
LRU_C = 8.0
SSD_HEADS = 8
SSD_HEADDIM = MIX_W // SSD_HEADS
SSD_GROUPS = 2
SSD_STATE = 128
SSD_CONV_DIM = MIX_W + 2 * SSD_GROUPS * SSD_STATE
N_EXPERTS = 16
EXPERTS_PER_GROUP = 4
D_EXPERT = D_MODEL // 2
MOE_SLOTS = 128

COL_U, COL_Q, COL_K, COL_V, COL_ZB = 0, 512, 1024, 1536, 2048
COL_XC, COL_GC, COL_ZD, COL_XBC, COL_GATES, COL_SMALL = 2560, 3072, 3584, 4096, 5120, 9216
PROJ_COLS = 9728
PROJ_COL_GROUPS = 4
SMALL_W = 128
LANE = 128
SUBLANE = 8
VMEM_LIMIT = 48 * 1024 * 1024


def _cparams(sem):
    return pltpu.CompilerParams(dimension_semantics=sem, vmem_limit_bytes=VMEM_LIMIT)


def _dot(a, b):
    return jnp.dot(a.astype(BF16), b.astype(BF16), preferred_element_type=F32)


def _dot_nt(a, b):
    return lax.dot_general(a.astype(BF16), b.astype(BF16), (((1,), (1,)), ((), ())),
                           preferred_element_type=F32)


def _dot_tn(a, b):
    return lax.dot_general(a.astype(BF16), b.astype(BF16), (((0,), (0,)), ((), ())),
                           preferred_element_type=F32)


def _split3(x):
    x1 = x.astype(BF16)
    r = x - x1.astype(F32)
    x2 = r.astype(BF16)
    x3 = (r - x2.astype(F32)).astype(BF16)
    return x1, x2, x3


def _dot01(m01, x):
    x1, x2, x3 = _split3(x)
    m = m01.astype(BF16)
    d = jnp.dot(m, x3, preferred_element_type=F32)
    d = d + jnp.dot(m, x2, preferred_element_type=F32)
    return d + jnp.dot(m, x1, preferred_element_type=F32)


def _dot01_nt(m01, x):
    x1, x2, x3 = _split3(x)
    m = m01.astype(BF16)
    dn = (((1,), (1,)), ((), ()))
    d = lax.dot_general(m, x3, dn, preferred_element_type=F32)
    d = d + lax.dot_general(m, x2, dn, preferred_element_type=F32)
    return d + lax.dot_general(m, x1, dn, preferred_element_type=F32)


def _dot3(a, b):
    a1 = a.astype(BF16)
    a2 = (a - a1.astype(F32)).astype(BF16)
    b1 = b.astype(BF16)
    b2 = (b - b1.astype(F32)).astype(BF16)
    d = jnp.dot(a1, b2, preferred_element_type=F32)
    d = d + jnp.dot(a2, b1, preferred_element_type=F32)
    return d + jnp.dot(a1, b1, preferred_element_type=F32)


def _sigmoid(x):
    return 1.0 / (1.0 + jnp.exp(-x))


def _silu(x):
    return x * _sigmoid(x)


def _gelu_tanh(x):
    return 0.5 * x * (1.0 + jnp.tanh(0.7978845608028654 * (x + 0.044715 * (x * x * x))))


def _softplus(x):
    return jnp.maximum(x, 0.0) + jnp.log1p(jnp.exp(-jnp.abs(x)))


def _log2(n):
    k = int(math.log2(n))
    assert (1 << k) == n
    return k


def _ada_kernel(c_ref, w_ref, b_ref, o_ref):
    c = c_ref[...]
    o_ref[...] = _dot(_silu(c), w_ref[...]) + b_ref[...]


def _ada(c_all, w_ada, b_ada):
    n = c_all.shape[0]
    tn = 1536
    return pl.pallas_call(
        _ada_kernel,
        out_shape=jax.ShapeDtypeStruct((DEPTH, n, 6 * D_MODEL), F32),
        grid=(DEPTH, 6 * D_MODEL // tn),
        in_specs=[pl.BlockSpec((n, D_MODEL), lambda l, j: (0, 0)),
                  pl.BlockSpec((None, D_MODEL, tn), lambda l, j: (l, 0, j)),
                  pl.BlockSpec((None, 1, tn), lambda l, j: (l, 0, j))],
        out_specs=pl.BlockSpec((None, n, tn), lambda l, j: (l, 0, j)),
        compiler_params=_cparams(("parallel", "parallel")),
        name="ada",
    )(c_all, w_ada, b_ada.reshape(DEPTH, 1, 6 * D_MODEL))


def _s5_prep_kernel(are_ref, aim_ref, ldt_ref, bre_ref, bim_ref, pwr_ref, pwi_ref, bbr_ref, bbi_ref):
    lam_re = are_ref[...]
    lam_im = aim_ref[...]
    dt = jnp.exp(ldt_ref[...])
    mag = jnp.exp(lam_re * dt)
    lb_re = mag * jnp.cos(lam_im * dt)
    lb_im = mag * jnp.sin(lam_im * dt)
    den = lam_re * lam_re + lam_im * lam_im
    f_re = ((lb_re - 1.0) * lam_re + lb_im * lam_im) / den
    f_im = (lb_im * lam_re - (lb_re - 1.0) * lam_im) / den
    b_re = bre_ref[...]
    b_im = bim_ref[...]
    bbr_ref[...] = f_re * b_re - f_im * b_im
    bbi_ref[...] = f_re * b_im + f_im * b_re
    p_re, p_im = lb_re, lb_im
    pwr_ref[0] = p_re
    pwi_ref[0] = p_im
    for k in range(1, SUBLANE):
        p_re, p_im = p_re * lb_re - p_im * lb_im, p_re * lb_im + p_im * lb_re
        pwr_ref[k] = p_re
        pwi_ref[k] = p_im


def _s5_prep(a_re, a_im, log_dt, b_re, b_im):
    g, p, c = S5_GROUPS, S5_STATE, S5_GROUP
    spec_a = pl.BlockSpec((None, g, 1, p), lambda l: (l, 0, 0, 0))
    spec_b = pl.BlockSpec((None, g, c, p), lambda l: (l, 0, 0, 0))
    spec_pw = pl.BlockSpec((None, SUBLANE, g, 1, p), lambda l: (l, 0, 0, 0, 0))
    return pl.pallas_call(
        _s5_prep_kernel,
        out_shape=(jax.ShapeDtypeStruct((DEPTH, SUBLANE, g, 1, p), F32),
                   jax.ShapeDtypeStruct((DEPTH, SUBLANE, g, 1, p), F32),
                   jax.ShapeDtypeStruct((DEPTH, g, c, p), F32),
                   jax.ShapeDtypeStruct((DEPTH, g, c, p), F32)),
        grid=(DEPTH,),
        in_specs=[spec_a, spec_a, pl.BlockSpec((None, g, 1, 1), lambda l: (l, 0, 0, 0)), spec_b, spec_b],
        out_specs=(spec_pw, spec_pw, spec_b, spec_b),
        compiler_params=_cparams(("parallel",)),
        name="s5_prep",
    )(a_re.reshape(DEPTH, g, 1, p), a_im.reshape(DEPTH, g, 1, p), log_dt.reshape(DEPTH, g, 1, 1),
      jnp.swapaxes(b_re, 2, 3), jnp.swapaxes(b_im, 2, 3))


def _block_diag(blocks):
    n, r, c = blocks.shape
    eye = jnp.eye(n, dtype=blocks.dtype)
    return jnp.einsum("grc,gh->grhc", blocks, eye).reshape(n * r, n * c)


def _s5_tables(pw_re, pw_im, bb_re, bb_im, c_re, c_im):
    pw_re = pw_re.reshape(SUBLANE, S5_FLAT)
    pw_im = pw_im.reshape(SUBLANE, S5_FLAT)
    row = jnp.arange(SUBLANE)[:, None]
    tabs = []
    for s in (1, 2, 4):
        tabs.append(jnp.where(row >= s, pw_re[s - 1][None, :], 0.0))
        tabs.append(jnp.where(row >= s, pw_im[s - 1][None, :], 0.0))
    tabs.append(pw_re)
    tabs.append(pw_im)
    tabs = jnp.stack(tabs)
    gpb = LANE // S5_GROUP
    nb = S5_GROUPS // gpb
    w_in = []
    c_out = []
    for j in range(nb):
        sl = slice(j * gpb, (j + 1) * gpb)
        w_in.append(jnp.concatenate([_block_diag(bb_re[sl]), _block_diag(bb_im[sl])], axis=1))
        c_out.append(_block_diag(jnp.swapaxes(c_re[sl], 1, 2)))
        c_out.append(-_block_diag(jnp.swapaxes(c_im[sl], 1, 2)))
    return tabs, jnp.stack(w_in).astype(BF16), jnp.stack(c_out).astype(BF16)


def _inproj_kernel(x_ref, nw_ref, sc_ref, sh_ref, w_ref, o_ref):
    x = x_ref[...]
    h = x * lax.rsqrt(jnp.mean(x * x, axis=-1, keepdims=True) + EPS) * nw_ref[...]
    h = h * (1.0 + sc_ref[...]) + sh_ref[...]
    o_ref[...] = jnp.dot(h.astype(BF16), w_ref[...], preferred_element_type=F32)


def _inproj(x, nw, sc, sh, w, tm, tiles_per_mod):
    t = x.shape[0]
    tn = PROJ_COLS // PROJ_COL_GROUPS
    mspec = pl.BlockSpec((None, sc.shape[1], D_MODEL), lambda j, i: (i // tiles_per_mod, 0, 0))
    return pl.pallas_call(
        _inproj_kernel,
        out_shape=jax.ShapeDtypeStruct((t, PROJ_COLS), F32),
        grid=(PROJ_COL_GROUPS, t // tm),
        in_specs=[pl.BlockSpec((tm, D_MODEL), lambda j, i: (i, 0)),
                  pl.BlockSpec((1, D_MODEL), lambda j, i: (0, 0)),
                  mspec, mspec,
                  pl.BlockSpec((D_MODEL, tn), lambda j, i: (0, j))],
        out_specs=pl.BlockSpec((tm, tn), lambda j, i: (i, j)),
        compiler_params=_cparams(("parallel", "parallel")),
        name="inproj",
    )(x, nw, sc, sh, w)


def _s5_kernel(u_ref, sre_ref, sim_ref, tab_ref, win_ref, cout_ref, d_ref, wglu_ref,
               y_ref, fre_ref, fim_ref, xr_ref, xi_ref, cr_ref, ci_ref, *, tc):
    c = pl.program_id(1)

    @pl.when(c == 0)
    def _():
        cr_ref[...] = sre_ref[...]
        ci_ref[...] = sim_ref[...]

    u = u_ref[...]
    ub = u.astype(BF16)
    nb = MIX_W // LANE
    half = S5_FLAT // nb
    for j in range(nb):
        bu = jnp.dot(ub[:, j * LANE:(j + 1) * LANE], win_ref[j], preferred_element_type=F32)
        xr_ref[:, j * half:(j + 1) * half] = bu[:, :half]
        xi_ref[:, j * half:(j + 1) * half] = bu[:, half:]

    wc = 512
    for cb in range(S5_FLAT // wc):
        cs = slice(cb * wc, (cb + 1) * wc)

        def body(g, carry, cs=cs):
            cr, ci = carry
            r0 = pl.multiple_of(g * SUBLANE, SUBLANE)
            br = xr_ref[pl.ds(r0, SUBLANE), cs]
            bi = xi_ref[pl.ds(r0, SUBLANE), cs]
            for lvl, s in enumerate((1, 2, 4)):
                ar = tab_ref[2 * lvl, :, cs]
                ai = tab_ref[2 * lvl + 1, :, cs]
                sr = pltpu.roll(br, s, axis=0)
                si = pltpu.roll(bi, s, axis=0)
                br, bi = br + (ar * sr - ai * si), bi + (ar * si + ai * sr)
            pr = tab_ref[6, :, cs]
            pi_ = tab_ref[7, :, cs]
            br, bi = br + (pr * cr - pi_ * ci), bi + (pr * ci + pi_ * cr)
            xr_ref[pl.ds(r0, SUBLANE), cs] = br
            xi_ref[pl.ds(r0, SUBLANE), cs] = bi
            return br[SUBLANE - 1:SUBLANE, :], bi[SUBLANE - 1:SUBLANE, :]

        cr, ci = lax.fori_loop(0, tc // SUBLANE, body, (cr_ref[:, cs], ci_ref[:, cs]))
        cr_ref[:, cs] = cr
        ci_ref[:, cs] = ci

    ys = []
    for j in range(nb):
        xr = xr_ref[:, j * half:(j + 1) * half].astype(BF16)
        xi = xi_ref[:, j * half:(j + 1) * half].astype(BF16)
        ys.append(jnp.dot(xr, cout_ref[2 * j], preferred_element_type=F32)
                  + jnp.dot(xi, cout_ref[2 * j + 1], preferred_element_type=F32))
    y = jnp.concatenate(ys, axis=1) + d_ref[...] * u
    y = _gelu_tanh(y)
    y = y * _sigmoid(jnp.dot(y.astype(BF16), wglu_ref[...], preferred_element_type=F32))
    y_ref[...] = y
    fre_ref[...] = cr_ref[...]
    fim_ref[...] = ci_ref[...]


def _s5_mixer(proj, st_re, st_im, tabs, w_in, c_out, d, w_glu, bsz, seq, tc):
    nc = seq // tc
    st_spec = pl.BlockSpec((None, 1, S5_FLAT), lambda b, c: (b, 0, 0))

    def const(shape):
        return pl.BlockSpec(shape, lambda b, c: (0,) * len(shape))

    y, f_re, f_im = pl.pallas_call(
        functools.partial(_s5_kernel, tc=tc),
        out_shape=(jax.ShapeDtypeStruct((bsz * seq, MIX_W), F32),
                   jax.ShapeDtypeStruct((bsz, 1, S5_FLAT), F32),
                   jax.ShapeDtypeStruct((bsz, 1, S5_FLAT), F32)),
        grid=(bsz, nc),
        in_specs=[pl.BlockSpec((tc, MIX_W), lambda b, c: (b * nc + c, COL_U // MIX_W)),
                  st_spec, st_spec,
                  const(tabs.shape), const(w_in.shape), const(c_out.shape),
                  const((1, MIX_W)), const((MIX_W, MIX_W))],
        out_specs=(pl.BlockSpec((tc, MIX_W), lambda b, c: (b * nc + c, 0)), st_spec, st_spec),
        scratch_shapes=[pltpu.VMEM((tc, S5_FLAT), F32), pltpu.VMEM((tc, S5_FLAT), F32),
                        pltpu.VMEM((1, S5_FLAT), F32), pltpu.VMEM((1, S5_FLAT), F32)],
        compiler_params=_cparams(("parallel", "arbitrary")),
        name="s5_mixer",
    )(proj, st_re.reshape(bsz, 1, S5_FLAT), st_im.reshape(bsz, 1, S5_FLAT), tabs, w_in, c_out, d, w_glu)
    shp = (bsz, S5_GROUPS, S5_STATE)
    return y, f_re.reshape(shp), f_im.reshape(shp)


def _conv_rows(xp_ref, cw_ref, q):
    out = cw_ref[0:1, :] * xp_ref[5:5 + q, :]
    for k in range(1, CONV_W):
        out = out + cw_ref[k:k + 1, :] * xp_ref[5 + k:5 + k + q, :]
    return out


def _conv_carry_in(c, xp_ref, buf_ref, q):
    @pl.when(c == 0)
    def _():
        xp_ref[5:8, :] = buf_ref[...]

    @pl.when(c > 0)
    def _():
        xp_ref[5:8, :] = xp_ref[q + 5:q + 8, :]


def _gdn_kernel(q_ref, k_ref, v_ref, z_ref, sm_ref, s0_ref, buf_ref, cw_ref, alog_ref, dtb_ref, nw_ref,
                o_ref, sfin_ref, nbuf_ref, xp_ref, s_ref, *, q, nc):
    c = pl.program_id(1)
    nh = GDN_HEADS
    r = nh * q
    lq = _log2(q)

    _conv_carry_in(c, xp_ref, buf_ref, q)

    @pl.when(c == 0)
    def _():
        s_ref[...] = s0_ref[...]

    xp_ref[8:8 + q, 0:MIX_W] = q_ref[...]
    xp_ref[8:8 + q, MIX_W:2 * MIX_W] = k_ref[...]
    xp_ref[8:8 + q, 2 * MIX_W:3 * MIX_W] = v_ref[...]
    qkv = _silu(_conv_rows(xp_ref, cw_ref, q))
    nbuf_ref[...] = xp_ref[q + 5:q + 8, :]

    def stack_heads(off):
        return jnp.concatenate([qkv[:, off + h * GDN_DK:off + (h + 1) * GDN_DK] for h in range(nh)], axis=0)

    qs = stack_heads(0)
    ks = stack_heads(MIX_W)
    vs = stack_heads(2 * MIX_W)
    qn = qs * lax.rsqrt(jnp.sum(qs * qs, axis=1, keepdims=True) + EPS) * (GDN_DK ** -0.5)
    kn = ks * lax.rsqrt(jnp.sum(ks * ks, axis=1, keepdims=True) + EPS)

    sm = sm_ref[...]
    g_t = -jnp.exp(alog_ref[...]) * _softplus(sm + dtb_ref[...])
    beta_t = _sigmoid(sm)
    ti = lax.broadcasted_iota(jnp.int32, (q, q), 0)
    si = lax.broadcasted_iota(jnp.int32, (q, q), 1)
    tril = jnp.where(si <= ti, 1.0, 0.0).astype(BF16)
    gc = _dot01(tril, g_t)
    lane = lax.broadcasted_iota(jnp.int32, (r, LANE), 1)
    hrow = lax.broadcasted_iota(jnp.int32, (r, LANE), 0) >> lq
    gm = jnp.where(lane == hrow, jnp.concatenate([gc] * nh, axis=0), 0.0)
    bm = jnp.where(lane == hrow + nh, jnp.concatenate([beta_t] * nh, axis=0), 0.0)
    gcol = jnp.sum(gm, axis=1, keepdims=True)
    bcol = jnp.sum(bm, axis=1, keepdims=True)
    ones = jnp.ones((SUBLANE, LANE), BF16)
    grow = _dot01_nt(ones, gm)[0:1, :]

    ri = lax.broadcasted_iota(jnp.int32, (r, r), 0)
    ci = lax.broadcasted_iota(jnp.int32, (r, r), 1)
    same = (ri >> lq) == (ci >> lq)
    decay = jnp.exp(jnp.where(same & (ci <= ri), gcol - grow, -jnp.inf))
    kk = _dot_nt(kn, kn)
    nmat = jnp.where(ci < ri, bcol * kk * decay, 0.0)
    eye = jnp.where(ri == ci, 1.0, 0.0)

    n0 = jnp.where((ri >> 3) == (ci >> 3), nmat, 0.0)
    n2 = _dot3(n0, n0)
    n4 = _dot3(n2, n2)
    tinv = _dot3(_dot3(eye - n0, eye + n2), eye + n4)
    ls = 3
    while (1 << ls) < q:
        off = jnp.where(((ri >> (ls + 1)) == (ci >> (ls + 1))) & ((ri >> ls) != (ci >> ls)), nmat, 0.0)
        tinv = tinv - _dot3(_dot3(tinv, off), tinv)
        ls += 1

    rhs = jnp.concatenate([vs * bcol, kn * (bcol * jnp.exp(gcol))], axis=1)
    sol = _dot3(tinv, rhs)
    sol_v = sol[:, :GDN_DK]
    sol_k = sol[:, GDN_DK:]
    qg = qn * jnp.exp(gcol)
    us = []
    outs = []
    for h in range(nh):
        rows = slice(h * q, (h + 1) * q)
        s_h = s_ref[h]
        us.append(sol_v[rows] - _dot(sol_k[rows], s_h))
        outs.append(_dot(qg[rows], s_h))
    u = jnp.concatenate(us, axis=0)
    attn = _dot_nt(qn, kn) * decay
    o_all = jnp.concatenate(outs, axis=0) + _dot(attn, u)

    z = z_ref[...]
    for h in range(nh):
        rows = slice(h * q, (h + 1) * q)
        g_last = gcol[(h + 1) * q - 1:(h + 1) * q, :]
        k_end = kn[rows] * jnp.exp(g_last - gcol[rows])
        s_ref[h] = s_ref[h] * jnp.exp(g_last) + _dot_tn(k_end, us[h])
        o_h = o_all[rows]
        o_h = o_h * lax.rsqrt(jnp.mean(o_h * o_h, axis=1, keepdims=True) + EPS) * nw_ref[...]
        o_ref[:, h * GDN_DK:(h + 1) * GDN_DK] = o_h * _silu(z[:, h * GDN_DK:(h + 1) * GDN_DK])

    @pl.when(c == nc - 1)
    def _():
        sfin_ref[...] = s_ref[...]


def _gdn_mixer(proj, s0, buf, conv_w, alog_row, dtb_row, norm_w, bsz, seq, q):
    nc = seq // q
    cb = 3 * MIX_W

    def col(off, w=MIX_W):
        return pl.BlockSpec((q, w), lambda b, c, o=off // w: (b * nc + c, o))

    def const(shape):
        return pl.BlockSpec(shape, lambda b, c: (0,) * len(shape))

    s_spec = pl.BlockSpec((None, GDN_HEADS, GDN_DK, GDN_DK), lambda b, c: (b, 0, 0, 0))
    b_spec = pl.BlockSpec((None, CONV_W - 1, cb), lambda b, c: (b, 0, 0))
    return pl.pallas_call(
        functools.partial(_gdn_kernel, q=q, nc=nc),
        out_shape=(jax.ShapeDtypeStruct((bsz * seq, MIX_W), F32),
                   jax.ShapeDtypeStruct((bsz, GDN_HEADS, GDN_DK, GDN_DK), F32),
                   jax.ShapeDtypeStruct((bsz, CONV_W - 1, cb), F32)),
        grid=(bsz, nc),
        in_specs=[col(COL_Q), col(COL_K), col(COL_V), col(COL_ZB), col(COL_SMALL, SMALL_W),
                  s_spec, b_spec, const((CONV_W, cb)), const((1, SMALL_W)), const((1, SMALL_W)),
                  const((1, GDN_DK))],
        out_specs=(pl.BlockSpec((q, MIX_W), lambda b, c: (b * nc + c, 0)), s_spec, b_spec),
        scratch_shapes=[pltpu.VMEM((q + 8, cb), F32), pltpu.VMEM((GDN_HEADS, GDN_DK, GDN_DK), F32)],
        compiler_params=_cparams(("parallel", "arbitrary")),
        name="gdn_mixer",
    )(proj, proj, proj, proj, proj, s0, buf, conv_w, alog_row, dtb_row, norm_w)


def _lru_kernel(x_ref, gate_ref, h0_ref, buf_ref, cw_ref, cb_ref, wax_ref, ba_ref, bx_ref, lam_ref,
                y_ref, hfin_ref, nbuf_ref, xp_ref, a_ref, b_ref, hc_ref, *, tc):
    c = pl.program_id(1)
    _conv_carry_in(c, xp_ref, buf_ref, tc)

    @pl.when(c == 0)
    def _():
        hc_ref[...] = h0_ref[...]

    xp_ref[8:8 + tc, :] = x_ref[...]
    xt = _conv_rows(xp_ref, cw_ref, tc) + cb_ref[...]
    nbuf_ref[...] = xp_ref[tc + 5:tc + 8, :]
    ra = jnp.dot(xt.astype(BF16), wax_ref[...], preferred_element_type=F32)
    rg = _sigmoid(ra[:, :MIX_W] + ba_ref[...])
    ig = _sigmoid(ra[:, MIX_W:] + bx_ref[...])
    log_a = -LRU_C * rg * _softplus(-lam_ref[...])
    a = jnp.exp(log_a)
    a_ref[...] = a
    b_ref[...] = jnp.sqrt(-jnp.tanh(log_a) * (a * a + 1.0)) * (ig * xt)

    row = lax.broadcasted_iota(jnp.int32, (SUBLANE, MIX_W), 0)

    def body(g, hc):
        r0 = pl.multiple_of(g * SUBLANE, SUBLANE)
        av = a_ref[pl.ds(r0, SUBLANE), :]
        bv = b_ref[pl.ds(r0, SUBLANE), :]
        for s in (1, 2, 4):
            a_s = pltpu.roll(av, s, axis=0)
            b_s = pltpu.roll(bv, s, axis=0)
            valid = row >= s
            bv = jnp.where(valid, av * b_s + bv, bv)
            av = jnp.where(valid, av * a_s, av)
        hv = bv + av * hc
        b_ref[pl.ds(r0, SUBLANE), :] = hv
        return hv[SUBLANE - 1:SUBLANE, :]

    hc = lax.fori_loop(0, tc // SUBLANE, body, hc_ref[...])
    hc_ref[...] = hc
    hfin_ref[...] = hc
    y_ref[...] = b_ref[...] * _gelu_tanh(gate_ref[...])


def _lru_mixer(proj, h0, buf, conv_w, conv_b, wax, b_a, b_x, lam, bsz, seq, tc):
    nc = seq // tc

    def const(shape):
        return pl.BlockSpec(shape, lambda b, c: (0,) * len(shape))

    h_spec = pl.BlockSpec((None, 1, MIX_W), lambda b, c: (b, 0, 0))
    b_spec = pl.BlockSpec((None, CONV_W - 1, MIX_W), lambda b, c: (b, 0, 0))
    y, hfin, nbuf = pl.pallas_call(
        functools.partial(_lru_kernel, tc=tc),
        out_shape=(jax.ShapeDtypeStruct((bsz * seq, MIX_W), F32),
                   jax.ShapeDtypeStruct((bsz, 1, MIX_W), F32),
                   jax.ShapeDtypeStruct((bsz, CONV_W - 1, MIX_W), F32)),
        grid=(bsz, nc),
        in_specs=[pl.BlockSpec((tc, MIX_W), lambda b, c: (b * nc + c, COL_XC // MIX_W)),
                  pl.BlockSpec((tc, MIX_W), lambda b, c: (b * nc + c, COL_GC // MIX_W)),
                  h_spec, b_spec, const((CONV_W, MIX_W)), const((1, MIX_W)),
                  const((MIX_W, 2 * MIX_W)), const((1, MIX_W)), const((1, MIX_W)), const((1, MIX_W))],
        out_specs=(pl.BlockSpec((tc, MIX_W), lambda b, c: (b * nc + c, 0)), h_spec, b_spec),
        scratch_shapes=[pltpu.VMEM((tc + 8, MIX_W), F32), pltpu.VMEM((tc, MIX_W), F32),
                        pltpu.VMEM((tc, MIX_W), F32), pltpu.VMEM((1, MIX_W), F32)],
        compiler_params=_cparams(("parallel", "arbitrary")),
        name="lru_mixer",
    )(proj, proj, h0.reshape(bsz, 1, MIX_W), buf, conv_w, conv_b, wax, b_a, b_x, lam)
    return y, hfin.reshape(bsz, MIX_W), nbuf


def _ssd_kernel(z_ref, xbc_ref, sm_ref, h0_ref, buf_ref, cw_ref, cb_ref, alog_ref, dtb_ref, dch_ref, nw_ref,
                y_ref, hfin_ref, nbuf_ref, xp_ref, h_ref, *, q, nc):
    c = pl.program_id(1)
    _conv_carry_in(c, xp_ref, buf_ref, q)

    @pl.when(c == 0)
    def _():
        h_ref[...] = h0_ref[...]

    xp_ref[8:8 + q, :] = xbc_ref[...]
    xbc = _silu(_conv_rows(xp_ref, cw_ref, q) + cb_ref[...])
    nbuf_ref[...] = xp_ref[q + 5:q + 8, :]
    xs = xbc[:, :MIX_W]
    gw = SSD_GROUPS * SSD_STATE
    bmat = xbc[:, MIX_W:MIX_W + gw]
    cmat = xbc[:, MIX_W + gw:]

    sm = sm_ref[...]
    dtv = _softplus(sm + dtb_ref[...])
    la = -jnp.exp(alog_ref[...]) * dtv
    ti = lax.broadcasted_iota(jnp.int32, (q, q), 0)
    si = lax.broadcasted_iota(jnp.int32, (q, q), 1)
    incl = si <= ti
    tril = jnp.where(incl, 1.0, 0.0).astype(BF16)
    gc = _dot01(tril, la)
    lane = lax.broadcasted_iota(jnp.int32, (q, LANE), 1)
    sel = jnp.where(lax.broadcasted_iota(jnp.int32, (SUBLANE, LANE), 1)
                    == lax.broadcasted_iota(jnp.int32, (SUBLANE, LANE), 0) + SUBLANE, 1.0, 0.0)
    gct = _dot01_nt(sel, gc)

    def head_col(arr, h):
        return jnp.sum(jnp.where(lane == SUBLANE + h, arr, 0.0), axis=1, keepdims=True)

    gcols = [head_col(gc, h) for h in range(SSD_HEADS)]
    dcols = [head_col(dtv, h) for h in range(SSD_HEADS)]
    lo = lane < SSD_HEADDIM
    prow = lax.broadcasted_iota(jnp.int32, (LANE, 1), 0) < SSD_HEADDIM
    cbs = [_dot_nt(cmat[:, g * SSD_STATE:(g + 1) * SSD_STATE], bmat[:, g * SSD_STATE:(g + 1) * SSD_STATE])
           for g in range(SSD_GROUPS)]
    ys = []
    for j in range(SSD_HEADS // 2):
        h0, h1 = 2 * j, 2 * j + 1
        g = h0 // (SSD_HEADS // SSD_GROUPS)
        bm_g = bmat[:, g * SSD_STATE:(g + 1) * SSD_STATE]
        cm_g = cmat[:, g * SSD_STATE:(g + 1) * SSD_STATE]
        xs_p = xs[:, j * LANE:(j + 1) * LANE]
        xdt = xs_p * jnp.where(lo, dcols[h0], dcols[h1])
        yh = []
        for h in (h0, h1):
            dec = jnp.exp(jnp.where(incl, gcols[h] - gct[h:h + 1, :], -jnp.inf))
            yh.append(_dot(cbs[g] * dec, xdt))
        y_p = jnp.where(lo, yh[0], yh[1])
        hp = h_ref[j]
        y_p = y_p + _dot_nt(cm_g, hp) * jnp.where(lo, jnp.exp(gcols[h0]), jnp.exp(gcols[h1]))
        gl0 = gcols[h0][q - 1:q, :]
        gl1 = gcols[h1][q - 1:q, :]
        w_end = jnp.where(lo, jnp.exp(gl0 - gcols[h0]) * dcols[h0], jnp.exp(gl1 - gcols[h1]) * dcols[h1])
        h_ref[j] = hp * jnp.where(prow, jnp.exp(gl0), jnp.exp(gl1)) + _dot_tn(xs_p * w_end, bm_g)
        ys.append(y_p)
    y = jnp.concatenate(ys, axis=1) + dch_ref[...] * xs
    y = y * _silu(z_ref[...])
    y_ref[...] = y * lax.rsqrt(jnp.mean(y * y, axis=1, keepdims=True) + EPS) * nw_ref[...]

    @pl.when(c == nc - 1)
    def _():
        hfin_ref[...] = h_ref[...]


def _ssd_mixer(proj, h0, buf, conv_w, conv_b, alog_row, dtb_row, d_chan, norm_w, bsz, seq, q):
    nc = seq // q
    npair = SSD_HEADS // 2

    def const(shape):
        return pl.BlockSpec(shape, lambda b, c: (0,) * len(shape))

    h_spec = pl.BlockSpec((None, npair, LANE, SSD_STATE), lambda b, c: (b, 0, 0, 0))
    b_spec = pl.BlockSpec((None, CONV_W - 1, SSD_CONV_DIM), lambda b, c: (b, 0, 0))
    y, hfin, nbuf = pl.pallas_call(
        functools.partial(_ssd_kernel, q=q, nc=nc),
        out_shape=(jax.ShapeDtypeStruct((bsz * seq, MIX_W), F32),
                   jax.ShapeDtypeStruct((bsz, npair, LANE, SSD_STATE), F32),
                   jax.ShapeDtypeStruct((bsz, CONV_W - 1, SSD_CONV_DIM), F32)),
        grid=(bsz, nc),
        in_specs=[pl.BlockSpec((q, MIX_W), lambda b, c: (b * nc + c, COL_ZD // MIX_W)),
                  pl.BlockSpec((q, SSD_CONV_DIM), lambda b, c: (b * nc + c, COL_XBC // SSD_CONV_DIM)),
                  pl.BlockSpec((q, SMALL_W), lambda b, c: (b * nc + c, COL_SMALL // SMALL_W)),
                  h_spec, b_spec, const((CONV_W, SSD_CONV_DIM)), const((1, SSD_CONV_DIM)),
                  const((1, SMALL_W)), const((1, SMALL_W)), const((1, MIX_W)), const((1, MIX_W))],
        out_specs=(pl.BlockSpec((q, MIX_W), lambda b, c: (b * nc + c, 0)), h_spec, b_spec),
        scratch_shapes=[pltpu.VMEM((q + 8, SSD_CONV_DIM), F32), pltpu.VMEM((npair, LANE, SSD_STATE), F32)],
        compiler_params=_cparams(("parallel", "arbitrary")),
        name="ssd_mixer",
    )(proj, proj, proj, h0.reshape(bsz, npair, LANE, SSD_STATE), buf, conv_w, conv_b,
      alog_row, dtb_row, d_chan, norm_w)
    return y, hfin.reshape(bsz, SSD_HEADS, SSD_HEADDIM, SSD_STATE), nbuf


def _merge_kernel(ya_ref, yb_ref, yc_ref, yd_ref, g0_ref, g1_ref, g2_ref, g3_ref, x_ref, gm_ref,
                  wbr_ref, wout_ref, nw_ref, sc_ref, sh_ref, rwt_ref, rb_ref,
                  x1_ref, h2_ref, rank_ref, wt_ref, cnt_ref, *, tm):
    merged = None
    for k, (y_ref, g_ref) in enumerate(((ya_ref, g0_ref), (yb_ref, g1_ref), (yc_ref, g2_ref), (yd_ref, g3_ref))):
        br = jnp.dot(y_ref[...].astype(BF16), wbr_ref[k], preferred_element_type=F32)
        term = _sigmoid(g_ref[...]) * br
        merged = term if merged is None else merged + term
    mix = jnp.dot(merged.astype(BF16), wout_ref[...], preferred_element_type=F32)
    x1 = x_ref[...] + gm_ref[...] * mix
    x1_ref[...] = x1
    h = x1 * lax.rsqrt(jnp.mean(x1 * x1, axis=-1, keepdims=True) + EPS) * nw_ref[...]
    h = h * (1.0 + sc_ref[...]) + sh_ref[...]
    hb = h.astype(BF16)
    h2_ref[...] = hb

    logits = lax.dot_general(rwt_ref[...], hb, (((1,), (1,)), ((), ())), preferred_element_type=F32)
    scores = _sigmoid(logits)
    sel = scores + rb_ref[...]
    ng = N_EXPERTS // EXPERTS_PER_GROUP
    rows = [sel[e:e + 1, :] for e in range(N_EXPERTS)]
    best = None
    gidx = None
    for g in range(ng):
        v = rows[g * EXPERTS_PER_GROUP:(g + 1) * EXPERTS_PER_GROUP]
        top2 = None
        for i in range(EXPERTS_PER_GROUP):
            for j in range(i + 1, EXPERTS_PER_GROUP):
                s = v[i] + v[j]
                top2 = s if top2 is None else jnp.maximum(top2, s)
        if best is None:
            best, gidx = top2, jnp.zeros_like(top2, dtype=jnp.int32)
        else:
            take = top2 > best
            gidx = jnp.where(take, g, gidx)
            best = jnp.where(take, top2, best)
    vals = []
    for i in range(EXPERTS_PER_GROUP):
        v = rows[i]
        for g in range(1, ng):
            v = jnp.where(gidx == g, rows[g * EXPERTS_PER_GROUP + i], v)
        vals.append(v)
    m1, i1 = vals[0], jnp.zeros_like(gidx)
    for i in range(1, EXPERTS_PER_GROUP):
        take = vals[i] > m1
        i1 = jnp.where(take, i, i1)
        m1 = jnp.where(take, vals[i], m1)
    m2, i2 = None, None
    for i in range(EXPERTS_PER_GROUP):
        cand = jnp.where(i1 == i, -jnp.inf, vals[i])
        if m2 is None:
            m2, i2 = cand, jnp.zeros_like(gidx)
        else:
            take = cand > m2
            i2 = jnp.where(take, i, i2)
            m2 = jnp.where(take, cand, m2)
    e1 = gidx * EXPERTS_PER_GROUP + i1
    e2 = gidx * EXPERTS_PER_GROUP + i2
    eid = lax.broadcasted_iota(jnp.int32, (N_EXPERTS, tm), 0)
    is1 = eid == e1
    is2 = eid == e2
    s1 = jnp.sum(jnp.where(is1, scores, 0.0), axis=0, keepdims=True)
    s2 = jnp.sum(jnp.where(is2, scores, 0.0), axis=0, keepdims=True)
    tot = s1 + s2
    wt_ref[...] = jnp.where(is1, s1 / tot, jnp.where(is2, s2 / tot, 0.0))
    chosen = is1 | is2
    ch = jnp.where(chosen, 1.0, 0.0)
    ti = lax.broadcasted_iota(jnp.int32, (tm, tm), 0)
    si = lax.broadcasted_iota(jnp.int32, (tm, tm), 1)
    before = jnp.where(ti < si, 1.0, 0.0).astype(BF16)
    rank = jnp.dot(ch.astype(BF16), before, preferred_element_type=F32)
    rank_ref[...] = jnp.where(chosen, rank, -1.0)
    cnt_ref[...] = jnp.sum(ch, axis=1, keepdims=True).astype(jnp.int32)


def _merge(ys, proj, x, g_m, wbr, wout, nw, sc, sh, rwt, rb, tm, tiles_per_mod):
    t = x.shape[0]
    nt = t // tm
    mspec = pl.BlockSpec((None, sc.shape[1], D_MODEL), lambda i: (i // tiles_per_mod, 0, 0))
    yspec = pl.BlockSpec((tm, MIX_W), lambda i: (i, 0))
    gspecs = [pl.BlockSpec((tm, D_MODEL), lambda i, o=COL_GATES // D_MODEL + k: (i, o)) for k in range(N_BRANCH)]
    xspec = pl.BlockSpec((tm, D_MODEL), lambda i: (i, 0))
    rspec = pl.BlockSpec((N_EXPERTS, tm), lambda i: (0, i))

    def const(shape):
        return pl.BlockSpec(shape, lambda i: (0,) * len(shape))

    return pl.pallas_call(
        functools.partial(_merge_kernel, tm=tm),
        out_shape=(jax.ShapeDtypeStruct((t, D_MODEL), F32),
                   jax.ShapeDtypeStruct((t, D_MODEL), BF16),
                   jax.ShapeDtypeStruct((N_EXPERTS, t), F32),
                   jax.ShapeDtypeStruct((N_EXPERTS, t), F32),
                   jax.ShapeDtypeStruct((nt, N_EXPERTS, 1), jnp.int32)),
        grid=(nt,),
        in_specs=[yspec] * 4 + gspecs + [xspec, mspec, const(wbr.shape), const(wout.shape),
                                         const((1, D_MODEL)), mspec, mspec,
                                         const((N_EXPERTS, D_MODEL)), const((N_EXPERTS, 1))],
        out_specs=(xspec, xspec, rspec, rspec, pl.BlockSpec((None, N_EXPERTS, 1), lambda i: (i, 0, 0))),
        compiler_params=_cparams(("parallel",)),
        name="merge_route",
    )(*ys, proj, proj, proj, proj, x, g_m, wbr, wout, nw, sc, sh, rwt, rb)


def _moe_kernel(cnt_ref, h_ref, rank_ref, wt_ref, x_ref, gf_ref, wg_ref, wu_ref, wd_ref, fnw_ref,
                o_ref, acc_ref, *, tm, final_norm):
    i = pl.program_id(0)
    e = pl.program_id(1)

    @pl.when(e == 0)
    def _():
        acc_ref[...] = jnp.zeros_like(acc_ref)

    cnt = cnt_ref[i * N_EXPERTS + e]
    nblk = lax.shift_right_logical(cnt + (MOE_SLOTS - 1), _log2(MOE_SLOTS))

    def block(b, carry):
        rank = rank_ref[pl.ds(e, 1), :]
        wrow = wt_ref[pl.ds(e, 1), :]
        slot = (b * MOE_SLOTS + lax.broadcasted_iota(jnp.int32, (MOE_SLOTS, 1), 0)).astype(F32)
        hit = rank == slot
        pick = jnp.where(hit, 1.0, 0.0).astype(BF16)
        xb = jnp.dot(pick, h_ref[...], preferred_element_type=F32).astype(BF16)
        gate = jnp.dot(xb, wg_ref[...], preferred_element_type=F32)
        up = jnp.dot(xb, wu_ref[...], preferred_element_type=F32)
        act = (_silu(gate) * up).astype(BF16)
        yb = jnp.dot(act, wd_ref[...], preferred_element_type=F32)
        w_slot = jnp.sum(jnp.where(hit, wrow, 0.0), axis=1, keepdims=True)
        acc_ref[...] += lax.dot_general(pick, (yb * w_slot).astype(BF16), (((0,), (0,)), ((), ())),
                                        preferred_element_type=F32)
        return carry

    lax.fori_loop(0, nblk, block, 0)

    @pl.when(e == N_EXPERTS - 1)
    def _():
        x2 = x_ref[...] + gf_ref[...] * acc_ref[...]
        if final_norm:
            x2 = x2 * lax.rsqrt(jnp.mean(x2 * x2, axis=-1, keepdims=True) + EPS) * fnw_ref[...]
        o_ref[...] = x2


def _moe(counts, h2, rank, wt, x1, g_f, wg, wu, wd, fnw, tm, tiles_per_mod, final_norm):
    t = x1.shape[0]
    nt = t // tm
    mspec = pl.BlockSpec((None, g_f.shape[1], D_MODEL), lambda i, e, c: (i // tiles_per_mod, 0, 0))
    xspec = pl.BlockSpec((tm, D_MODEL), lambda i, e, c: (i, 0))
    rspec = pl.BlockSpec((N_EXPERTS, tm), lambda i, e, c: (0, i))
    grid_spec = pltpu.PrefetchScalarGridSpec(
        num_scalar_prefetch=1,
        grid=(nt, N_EXPERTS),
        in_specs=[xspec, rspec, rspec, xspec, mspec,
                  pl.BlockSpec((None, D_MODEL, D_EXPERT), lambda i, e, c: (e, 0, 0)),
                  pl.BlockSpec((None, D_MODEL, D_EXPERT), lambda i, e, c: (e, 0, 0)),
                  pl.BlockSpec((None, D_EXPERT, D_MODEL), lambda i, e, c: (e, 0, 0)),
                  pl.BlockSpec((1, D_MODEL), lambda i, e, c: (0, 0))],
        out_specs=xspec,
        scratch_shapes=[pltpu.VMEM((tm, D_MODEL), F32)],
    )
    return pl.pallas_call(
        functools.partial(_moe_kernel, tm=tm, final_norm=final_norm),
        out_shape=jax.ShapeDtypeStruct((t, D_MODEL), F32),
        grid_spec=grid_spec,
        compiler_params=_cparams(("parallel", "arbitrary")),
        name="moe",
    )(counts, h2, rank, wt, x1, g_f, wg, wu, wd, fnw)


def _small_row(vals, off):
    return jnp.zeros((1, SMALL_W), F32).at[0, off:off + vals.shape[0]].set(vals)


def _prep_layer(l, p, s5p):
    w_in = p["w_in"][l]
    pad = PROJ_COLS - COL_SMALL - 16
    w_perm = jnp.concatenate([w_in[:, 0:2560], w_in[:, 2568:5128], w_in[:, 5136:9232],
                              w_in[:, 2560:2568], w_in[:, 5128:5136],
                              jnp.zeros((D_MODEL, pad), F32)], axis=1).astype(BF16)
    tabs, s5_win, s5_cout = _s5_tables(s5p[0][l], s5p[1][l], s5p[2][l], s5p[3][l], p["s5_c_re"][l], p["s5_c_im"][l])
    wax = jnp.concatenate([_block_diag(p["lru_w_a"][l]), _block_diag(p["lru_w_x"][l])], axis=1).astype(BF16)
    return dict(
        w_in=w_perm,
        s5=(tabs, s5_win, s5_cout, p["s5_d"][l].reshape(1, MIX_W), p["s5_w_glu"][l].astype(BF16)),
        gdn=(p["gdn_conv_w"][l], _small_row(p["gdn_a_log"][l], 0), _small_row(p["gdn_dt_bias"][l], 0),
             p["gdn_norm_w"][l].reshape(1, GDN_DK)),
        lru=(p["lru_conv_w"][l], p["lru_conv_b"][l].reshape(1, MIX_W), wax, p["lru_b_a"][l].reshape(1, MIX_W),
             p["lru_b_x"][l].reshape(1, MIX_W), p["lru_lambda"][l].reshape(1, MIX_W)),
        ssd=(p["ssd_conv_w"][l], p["ssd_conv_b"][l].reshape(1, SSD_CONV_DIM),
             _small_row(p["ssd_a_log"][l], SUBLANE), _small_row(p["ssd_dt_bias"][l], SUBLANE),
             jnp.repeat(p["ssd_d"][l], SSD_HEADDIM).reshape(1, MIX_W), p["ssd_norm_w"][l].reshape(1, MIX_W)),
        wbr=p["w_branch"][l].astype(BF16),
        wout=p["w_out"][l].astype(BF16),
        norm_mix=p["norm_mix_w"][l].reshape(1, D_MODEL),
        norm_moe=p["norm_moe_w"][l].reshape(1, D_MODEL),
        moe=(p["moe_w_gate"][l].astype(BF16), p["moe_w_up"][l].astype(BF16), p["moe_w_down"][l].astype(BF16)),
    )


def _group_cfg(bsz, seq):
    if seq >= 512:
        return dict(tm=512, per_row_mod=False, tc_s5=256, q_gdn=64, tc_lru=256, q_ssd=128)
    t = bsz * seq
    return dict(tm=t, per_row_mod=True, tc_s5=seq, q_gdn=seq, tc_lru=seq, q_ssd=seq)


def _trunk(x, mod, states, layers, rwt, rb, fnw, bsz, seq):
    cfg = _group_cfg(bsz, seq)
    tm = cfg["tm"]
    t = bsz * seq
    x = x.reshape(t, D_MODEL)
    new_states = []
    for l in range(DEPTH):
        lp = layers[l]
        m = mod[l]
        if cfg["per_row_mod"]:
            parts = [jnp.repeat(m[:, k * D_MODEL:(k + 1) * D_MODEL], seq, axis=0).reshape(1, t, D_MODEL)
                     for k in range(6)]
            tiles_per_mod = t // tm
        else:
            parts = [m[:, k * D_MODEL:(k + 1) * D_MODEL].reshape(bsz, 1, D_MODEL) for k in range(6)]
            tiles_per_mod = seq // tm
        sh_m, sc_m, g_m, sh_f, sc_f, g_f = parts
        s5_re0, s5_im0, gdn_s0, gdn_buf0, lru_h0, lru_buf0, ssd_h0, ssd_buf0 = states[l]
        proj = _inproj(x, lp["norm_mix"], sc_m, sh_m, lp["w_in"], tm, tiles_per_mod)
        y_a, s5_re, s5_im = _s5_mixer(proj, s5_re0, s5_im0, *lp["s5"], bsz, seq, cfg["tc_s5"])
        y_b, gdn_s, gdn_buf = _gdn_mixer(proj, gdn_s0, gdn_buf0, *lp["gdn"], bsz, seq, cfg["q_gdn"])
        y_c, lru_h, lru_buf = _lru_mixer(proj, lru_h0, lru_buf0, *lp["lru"], bsz, seq, cfg["tc_lru"])
        y_d, ssd_h, ssd_buf = _ssd_mixer(proj, ssd_h0, ssd_buf0, *lp["ssd"], bsz, seq, cfg["q_ssd"])
        x1, h2, rank, wt, cnt = _merge((y_a, y_b, y_c, y_d), proj, x, g_m, lp["wbr"], lp["wout"],
                                       lp["norm_moe"], sc_f, sh_f, rwt, rb, tm, tiles_per_mod)
        x = _moe(cnt.reshape(-1), h2, rank, wt, x1, g_f, *lp["moe"], fnw, tm, tiles_per_mod,
                 final_norm=(l == DEPTH - 1))
        new_states.append((s5_re, s5_im, gdn_s, gdn_buf, lru_h, lru_buf, ssd_h, ssd_buf))
    stacked = tuple(jnp.stack([new_states[l][i] for l in range(DEPTH)]) for i in range(8))
    return x.reshape(bsz, seq, D_MODEL), stacked


def _zero_states(bsz):
    return (jnp.zeros((bsz, S5_GROUPS, S5_STATE), F32), jnp.zeros((bsz, S5_GROUPS, S5_STATE), F32),
            jnp.zeros((bsz, GDN_HEADS, GDN_DK, GDN_DK), F32), jnp.zeros((bsz, CONV_W - 1, 3 * MIX_W), F32),
            jnp.zeros((bsz, MIX_W), F32), jnp.zeros((bsz, CONV_W - 1, MIX_W), F32),
            jnp.zeros((bsz, SSD_HEADS, SSD_HEADDIM, SSD_STATE), F32),
            jnp.zeros((bsz, CONV_W - 1, SSD_CONV_DIM), F32))


def kernel(x_prompt, x_sample, c_prompt, c_sample, state_s5_re, state_s5_im, state_gdn, cache_gdn_conv, state_lru, cache_lru_conv, state_ssd, cache_ssd_conv, norm_mix_w, norm_moe_w, norm_final_w, w_ada, b_ada, w_in, s5_a_re, s5_a_im, s5_b_re, s5_b_im, s5_c_re, s5_c_im, s5_log_dt, s5_d, s5_w_glu, gdn_conv_w, gdn_a_log, gdn_dt_bias, gdn_norm_w, lru_conv_w, lru_conv_b, lru_w_a, lru_b_a, lru_w_x, lru_b_x, lru_lambda, ssd_conv_w, ssd_conv_b, ssd_a_log, ssd_dt_bias, ssd_d, ssd_norm_w, w_branch, w_out, router_w, router_bias, moe_w_gate, moe_w_up, moe_w_down):
    p = dict(w_in=w_in, s5_c_re=s5_c_re, s5_c_im=s5_c_im, s5_d=s5_d, s5_w_glu=s5_w_glu,
             gdn_conv_w=gdn_conv_w, gdn_a_log=gdn_a_log, gdn_dt_bias=gdn_dt_bias, gdn_norm_w=gdn_norm_w,
             lru_conv_w=lru_conv_w, lru_conv_b=lru_conv_b, lru_w_a=lru_w_a, lru_b_a=lru_b_a,
             lru_w_x=lru_w_x, lru_b_x=lru_b_x, lru_lambda=lru_lambda,
             ssd_conv_w=ssd_conv_w, ssd_conv_b=ssd_conv_b, ssd_a_log=ssd_a_log, ssd_dt_bias=ssd_dt_bias,
             ssd_d=ssd_d, ssd_norm_w=ssd_norm_w, w_branch=w_branch, w_out=w_out,
             norm_mix_w=norm_mix_w, norm_moe_w=norm_moe_w,
             moe_w_gate=moe_w_gate, moe_w_up=moe_w_up, moe_w_down=moe_w_down)
    bp, lp_, _ = x_prompt.shape
    bs, ls, _ = x_sample.shape
    s5p = _s5_prep(s5_a_re, s5_a_im, s5_log_dt, s5_b_re, s5_b_im)
    layers = [_prep_layer(l, p, s5p) for l in range(DEPTH)]
    mod = _ada(jnp.concatenate([c_prompt, c_sample], axis=0), w_ada, b_ada)
    rwt = router_w.T.astype(BF16)
    rb = router_bias.reshape(N_EXPERTS, 1)
    fnw = norm_final_w.reshape(1, D_MODEL)
    prompt_states = [_zero_states(bp) for _ in range(DEPTH)]
    cache = (state_s5_re, state_s5_im, state_gdn, cache_gdn_conv, state_lru, cache_lru_conv, state_ssd, cache_ssd_conv)
    sample_states = [tuple(s[l] for s in cache) for l in range(DEPTH)]
    y_p, st_p = _trunk(x_prompt, mod[:, :bp], prompt_states, layers, rwt, rb, fnw, bp, lp_)
    y_s, st_s = _trunk(x_sample, mod[:, bp:], sample_states, layers, rwt, rb, fnw, bs, ls)
    return (y_p, y_s) + st_p + st_s
```
